```python
import math
import jax, jax.numpy as jnp
from jax import lax
import numpy as np

D_MODEL = 1024
BATCH = 2
SEQ = 8192
DEPTH = 4

N_MIXERS = 3
NORM_EPS = 1e-6
D_FF = ((-(-8 * D_MODEL // 3) + 255) // 256) * 256

D_RNN = 5 * D_MODEL // 4
LRU_HEADS = 10
LRU_HEAD_DIM = D_RNN // LRU_HEADS
CONV_WIDTH = 4
LRU_C = 8.0

NSA_HEADS = 16
NSA_KV_GROUPS = 4
NSA_HEAD_DIM = 64
CMP_LEN = 32
CMP_STRIDE = 16
SEL_LEN = 64
SEL_TOPN = 16
WINDOW = 512
Q_BLOCK = 128
FORCE_SCORE = 1e4
NSA_PROJ = NSA_HEADS * NSA_HEAD_DIM + 6 * NSA_KV_GROUPS * NSA_HEAD_DIM + 3 * NSA_HEADS

REL_BUCKETS = 32
REL_MAX_DIST = 128

GLA_HEADS = 4
GLA_DK = D_MODEL // 2
GLA_DV = D_MODEL
GLA_GATE_RANK = 16
GLA_TAU = 16.0
GLA_CHUNK = 64
GLA_PROJ = 2 * GLA_DK + 2 * GLA_DV + GLA_GATE_RANK

N_A = (DEPTH + 2) // 3
N_B = (DEPTH + 1) // 3
N_C = DEPTH // 3

kernel_name = 'hybrid_rglru_nsa_gla_interleaved'


def rmsnorm(x, g):
    xf = x.astype(jnp.float32)
    y = xf * lax.rsqrt(jnp.mean(xf * xf, axis=-1, keepdims=True) + NORM_EPS) * g.astype(jnp.float32)
    return y.astype(x.dtype)


def swiglu(h, w_gate, w_up, w_down):
    return (jax.nn.silu(h @ w_gate) * (h @ w_up)) @ w_down


def masked_softmax(logits, mask):
    lg = jnp.where(mask, logits.astype(jnp.float32), -jnp.inf)
    m = jnp.max(lg, axis=-1, keepdims=True)
    m = jnp.where(jnp.isfinite(m), m, 0.0)
    e = jnp.exp(lg - m)
    s = jnp.sum(e, axis=-1, keepdims=True)
    return e / jnp.where(s > 0, s, 1.0)


def rel_bucket(dist):
    n = jnp.maximum(dist, 0)
    max_exact = REL_BUCKETS // 2
    nf = jnp.maximum(n, 1).astype(jnp.float32)
    large = max_exact + (jnp.log(nf / max_exact) / math.log(REL_MAX_DIST / max_exact)
                         * (REL_BUCKETS - max_exact)).astype(jnp.int32)
    large = jnp.minimum(large, REL_BUCKETS - 1)
    return jnp.where(n < max_exact, n, large)


def causal_depthwise_conv(x, w, b):
    S = x.shape[1]
    xp = jnp.pad(x, ((0, 0), (CONV_WIDTH - 1, 0), (0, 0)))
    y = b
    for k in range(CONV_WIDTH):
        y = y + xp[:, k:k + S] * w[k]
    return y


def _linear_combine(left, right):
    a_l, b_l = left
    a_r, b_r = right
    return a_l * a_r, a_r * b_l + b_r


def rglru_mixer(h, w_in, conv_w, conv_b, w_a, b_a, w_x, b_x, lam, w_out):
    B, S, _ = h.shape
    gate_in, x_in = jnp.split(h @ w_in, 2, axis=-1)
    xc = causal_depthwise_conv(x_in, conv_w, conv_b)
    xh = xc.reshape(B, S, LRU_HEADS, LRU_HEAD_DIM)
    r = jax.nn.sigmoid((jnp.einsum('bshi,hij->bshj', xh, w_a).reshape(B, S, D_RNN) + b_a).astype(jnp.float32))
    i = jax.nn.sigmoid((jnp.einsum('bshi,hij->bshj', xh, w_x).reshape(B, S, D_RNN) + b_x).astype(jnp.float32))
    log_a = -LRU_C * r * jax.nn.softplus(-lam.astype(jnp.float32))
    a = jnp.exp(log_a)
    u = jnp.sqrt(-jnp.expm1(2.0 * log_a)) * (i * xc.astype(jnp.float32))
    _, hs = lax.associative_scan(_linear_combine, (a, u), axis=1)
    y = jax.nn.gelu(gate_in, approximate=True) * hs.astype(h.dtype)
    return y @ w_out


def nsa_compress(kv, pos, w1, b1, w2, b2):
    B, S, G, dh = kv.shape
    nb = (S - CMP_LEN) // CMP_STRIDE + 1
    idx = jnp.arange(nb)[:, None] * CMP_STRIDE + jnp.arange(CMP_LEN)[None, :]
    blocks = kv[:, idx] + pos[None, None, :, None, :]
    flat = jnp.moveaxis(blocks, 3, 2).reshape(B, nb, G, CMP_LEN * dh)
    return jax.nn.gelu(flat @ w1 + b1) @ w2 + b2


def cmp_to_sel_map(nb_cmp, nb_sel):
    start = jnp.arange(nb_cmp)[:, None] * CMP_STRIDE
    sel_start = jnp.arange(nb_sel)[None, :] * SEL_LEN
    return ((start < sel_start + SEL_LEN) & (start + CMP_LEN > sel_start)).astype(jnp.float32)


def nsa_mixer(h, w_in, cmp_pos, cmp_w1, cmp_b1, cmp_w2, cmp_b2, w_out, rel_bias):
    B, S, _ = h.shape
    G, HG, DH = NSA_KV_GROUPS, NSA_HEADS // NSA_KV_GROUPS, NSA_HEAD_DIM
    qw, kvw = NSA_HEADS * DH, NSA_KV_GROUPS * DH
    proj = h @ w_in
    q = proj[..., :qw].reshape(B, S, G, HG, DH) * DH ** -0.5
    kv = proj[..., qw:qw + 6 * kvw].reshape(B, S, 6, G, DH)
    k_c, v_c, k_s, v_s, k_w, v_w = [kv[:, :, j] for j in range(6)]
    gates = jax.nn.sigmoid(proj[..., qw + 6 * kvw:].reshape(B, S, G, HG, 3))

    kc = nsa_compress(k_c, cmp_pos[0], cmp_w1[0], cmp_b1[0], cmp_w2[0], cmp_b2[0])
    vc = nsa_compress(v_c, cmp_pos[1], cmp_w1[1], cmp_b1[1], cmp_w2[1], cmp_b2[1])
    nb_cmp = kc.shape[1]
    nb_sel = S // SEL_LEN
    n_sel = min(SEL_TOPN, nb_sel)
    sel_map = cmp_to_sel_map(nb_cmp, nb_sel)
    cmp_end = jnp.arange(nb_cmp) * CMP_STRIDE + CMP_LEN - 1

    def to_blocks(t):
        return jnp.moveaxis(t.reshape(B, nb_sel, SEL_LEN, G, DH), 3, 1).reshape(B, G, nb_sel, SEL_LEN * DH)

    ks_blk, vs_blk = to_blocks(k_s), to_blocks(v_s)
    kw_pad = jnp.pad(k_w, ((0, 0), (WINDOW, 0), (0, 0), (0, 0)))
    vw_pad = jnp.pad(v_w, ((0, 0), (WINDOW, 0), (0, 0), (0, 0)))
    table_g = jnp.transpose(rel_bias.reshape(REL_BUCKETS, G, HG), (1, 0, 2))
    blk_ids = jnp.arange(nb_sel)
    g_ids = jnp.arange(G)[None, :, None, None]

    def head_bias(dist):
        return jnp.transpose(rel_bias[rel_bucket(dist)].reshape(*dist.shape, G, HG), (2, 3, 0, 1))

    def query_block(qb):
        t0 = qb * Q_BLOCK
        tpos = t0 + jnp.arange(Q_BLOCK)
        qi = lax.dynamic_slice_in_dim(q, t0, Q_BLOCK, axis=1)
        gi = lax.dynamic_slice_in_dim(gates, t0, Q_BLOCK, axis=1)
        dist_c = tpos[:, None] - cmp_end[None, :]
        lg_c = jnp.einsum('bqghd,bngd->bghqn', qi, kc) + head_bias(dist_c)
        p_c = masked_softmax(lg_c, dist_c >= 0)
        o_c = jnp.einsum('bghqn,bngd->bqghd', p_c.astype(vc.dtype), vc)
        imp = jnp.einsum('bghqn,nm->bgqm', p_c, sel_map)
        cur = tpos // SEL_LEN
        valid = blk_ids[None, :] * SEL_LEN <= tpos[:, None]
        forced = (blk_ids[None, :] == 0) | (blk_ids[None, :] == cur[:, None]) | (blk_ids[None, :] == cur[:, None] - 1)
        score = jnp.where(valid, imp + jnp.where(forced, FORCE_SCORE, 0.0), -jnp.inf)
        _, idx = lax.top_k(score, n_sel)
        flat_idx = idx.reshape(B, G, Q_BLOCK * n_sel, 1)
        k_sel = jnp.take_along_axis(ks_blk, flat_idx, axis=2).reshape(B, G, Q_BLOCK, n_sel * SEL_LEN, DH)
        v_sel = jnp.take_along_axis(vs_blk, flat_idx, axis=2).reshape(B, G, Q_BLOCK, n_sel * SEL_LEN, DH)
        key_pos = (idx[..., None] * SEL_LEN + jnp.arange(SEL_LEN)).reshape(B, G, Q_BLOCK, n_sel * SEL_LEN)
        dist_s = tpos[None, None, :, None] - key_pos
        bias_s = jnp.moveaxis(table_g[g_ids, rel_bucket(dist_s)], -1, 3)
        lg_s = jnp.einsum('bqghd,bgqkd->bgqhk', qi, k_sel) + bias_s
        p_s = masked_softmax(lg_s, (dist_s >= 0)[:, :, :, None, :])
        o_s = jnp.einsum('bgqhk,bgqkd->bqghd', p_s.astype(v_sel.dtype), v_sel)
        kw = lax.dynamic_slice_in_dim(kw_pad, t0, Q_BLOCK + WINDOW, axis=1)
        vw = lax.dynamic_slice_in_dim(vw_pad, t0, Q_BLOCK + WINDOW, axis=1)
        key_pos_w = t0 - WINDOW + jnp.arange(Q_BLOCK + WINDOW)
        dist_w = tpos[:, None] - key_pos_w[None, :]
        mask_w = (dist_w >= 0) & (dist_w < WINDOW) & (key_pos_w[None, :] >= 0)
        lg_w = jnp.einsum('bqghd,bkgd->bghqk', qi, kw) + head_bias(dist_w)
        p_w = masked_softmax(lg_w, mask_w)
        o_w = jnp.einsum('bghqk,bkgd->bqghd', p_w.astype(vw.dtype), vw)
        return gi[..., 0:1] * o_c + gi[..., 1:2] * o_s + gi[..., 2:3] * o_w

    o = lax.map(query_block, jnp.arange(S // Q_BLOCK))
    o = jnp.moveaxis(o, 0, 1).reshape(B, S, NSA_HEADS * DH)
    return o @ w_out


def gla_mixer(h, w_in, w_g2, b_g2, norm_g, w_out):
    B, S, _ = h.shape
    H, C = GLA_HEADS, GLA_CHUNK
    dk, dv = GLA_DK // H, GLA_DV // H
    N = S // C
    proj = h @ w_in
    q, k, v, r, g_low = jnp.split(proj, [GLA_DK, 2 * GLA_DK, 2 * GLA_DK + GLA_DV, 2 * GLA_DK + 2 * GLA_DV], axis=-1)
    log_alpha = jax.nn.log_sigmoid((g_low @ w_g2 + b_g2).astype(jnp.float32)) / GLA_TAU

    def chunks(t, d):
        return jnp.transpose(t.astype(jnp.float32).reshape(B, N, C, H, d), (0, 3, 1, 2, 4))

    q = chunks(q, dk) * dk ** -0.5
    k, v, la = chunks(k, dk), chunks(v, dv), chunks(log_alpha, dk)
    b = jnp.cumsum(la, axis=3)
    b_last = b[:, :, :, -1:, :]
    q_dec = q * jnp.exp(b)
    k_inv = k * jnp.exp(-b)
    k_end = k * jnp.exp(b_last - b)
    causal = jnp.tril(jnp.ones((C, C), dtype=bool))
    att = jnp.where(causal, jnp.einsum('bhncd,bhnjd->bhncj', q_dec, k_inv), 0.0)
    o_intra = jnp.einsum('bhncj,bhnje->bhnce', att, v)

    def step(state, inp):
        q_c, k_c, v_c, dec = inp
        o = jnp.einsum('bhcd,bhde->bhce', q_c, state)
        state = state * dec[..., None] + jnp.einsum('bhcd,bhce->bhde', k_c, v_c)
        return state, o

    xs = (jnp.moveaxis(q_dec, 2, 0), jnp.moveaxis(k_end, 2, 0), jnp.moveaxis(v, 2, 0),
          jnp.moveaxis(jnp.exp(b_last[:, :, :, 0, :]), 2, 0))
    _, o_inter = lax.scan(step, jnp.zeros((B, H, dk, dv), jnp.float32), xs)
    o = o_intra + jnp.moveaxis(o_inter, 0, 2)
    o = jnp.transpose(o, (0, 2, 3, 1, 4)).reshape(B, S, H, dv)
    o = o * lax.rsqrt(jnp.mean(o * o, axis=-1, keepdims=True) + NORM_EPS) * norm_g.astype(jnp.float32)
    o = o.reshape(B, S, GLA_DV).astype(h.dtype) * jax.nn.silu(r)
    return o @ w_out


def setup_inputs(seed: int = 0) -> dict:
    key = jax.random.key(seed)
    keys = iter(jax.random.split(key, 40))
    f32 = jnp.float32

    def nrm(shape, fan_in, scale=1.0):
        return jax.random.normal(next(keys), shape, f32) * (scale * fan_in ** -0.5)

    def gain(shape):
        return 1.0 + 0.02 * jax.random.normal(next(keys), shape, f32)

    def small(shape, s=0.01):
        return s * jax.random.normal(next(keys), shape, f32)

    D = D_MODEL
    x = jax.random.normal(next(keys), (BATCH, SEQ, D), f32)
    rel_bias = small((REL_BUCKETS, NSA_HEADS), 0.2)
    final_norm = gain((D,))
    mix_norm = gain((DEPTH, D))
    ffn_norm = gain((DEPTH, D))
    ffn_w_gate = nrm((DEPTH, D, D_FF), D)
    ffn_w_up = nrm((DEPTH, D, D_FF), D)
    ffn_w_down = nrm((DEPTH, D_FF, D), D_FF)
    lru_w_in = nrm((N_A, D, 2 * D_RNN), D)
    lru_conv_w = nrm((N_A, CONV_WIDTH, D_RNN), CONV_WIDTH)
    lru_conv_b = small((N_A, D_RNN))
    lru_w_a = nrm((N_A, LRU_HEADS, LRU_HEAD_DIM, LRU_HEAD_DIM), LRU_HEAD_DIM)
    lru_b_a = small((N_A, D_RNN))
    lru_w_x = nrm((N_A, LRU_HEADS, LRU_HEAD_DIM, LRU_HEAD_DIM), LRU_HEAD_DIM)
    lru_b_x = small((N_A, D_RNN))
    u = jax.random.uniform(next(keys), (N_A, D_RNN), f32, 0.9, 0.999)
    p = u ** (1.0 / LRU_C)
    lru_lam = jnp.log(p) - jnp.log1p(-p)
    lru_w_out = nrm((N_A, D_RNN, D), D_RNN)
    nsa_w_in = nrm((N_B, D, NSA_PROJ), D)
    nsa_cmp_pos = small((N_B, 2, CMP_LEN, NSA_HEAD_DIM), 0.1)
    nsa_cmp_w1 = nrm((N_B, 2, CMP_LEN * NSA_HEAD_DIM, NSA_HEAD_DIM), CMP_LEN * NSA_HEAD_DIM)
    nsa_cmp_b1 = small((N_B, 2, NSA_HEAD_DIM))
    nsa_cmp_w2 = nrm((N_B, 2, NSA_HEAD_DIM, NSA_HEAD_DIM), NSA_HEAD_DIM)
    nsa_cmp_b2 = small((N_B, 2, NSA_HEAD_DIM))
    nsa_w_out = nrm((N_B, NSA_HEADS * NSA_HEAD_DIM, D), NSA_HEADS * NSA_HEAD_DIM)
    gla_w_in = nrm((N_C, D, GLA_PROJ), D)
    gla_w_g2 = nrm((N_C, GLA_GATE_RANK, GLA_DK), GLA_GATE_RANK)
    gla_b_g2 = small((N_C, GLA_DK), 0.1)
    gla_norm = gain((N_C, GLA_DV // GLA_HEADS))
    gla_w_out = nrm((N_C, GLA_DV, D), GLA_DV)
    return {
        'x': x, 'rel_bias': rel_bias, 'final_norm': final_norm, 'mix_norm': mix_norm, 'ffn_norm': ffn_norm,
        'ffn_w_gate': ffn_w_gate, 'ffn_w_up': ffn_w_up, 'ffn_w_down': ffn_w_down,
        'lru_w_in': lru_w_in, 'lru_conv_w': lru_conv_w, 'lru_conv_b': lru_conv_b, 'lru_w_a': lru_w_a,
        'lru_b_a': lru_b_a, 'lru_w_x': lru_w_x, 'lru_b_x': lru_b_x, 'lru_lam': lru_lam, 'lru_w_out': lru_w_out,
        'nsa_w_in': nsa_w_in, 'nsa_cmp_pos': nsa_cmp_pos, 'nsa_cmp_w1': nsa_cmp_w1, 'nsa_cmp_b1': nsa_cmp_b1,
        'nsa_cmp_w2': nsa_cmp_w2, 'nsa_cmp_b2': nsa_cmp_b2, 'nsa_w_out': nsa_w_out,
        'gla_w_in': gla_w_in, 'gla_w_g2': gla_w_g2, 'gla_b_g2': gla_b_g2, 'gla_norm': gla_norm, 'gla_w_out': gla_w_out,
    }


def reference(x, rel_bias, final_norm, mix_norm, ffn_norm, ffn_w_gate, ffn_w_up, ffn_w_down,
              lru_w_in, lru_conv_w, lru_conv_b, lru_w_a, lru_b_a, lru_w_x, lru_b_x, lru_lam, lru_w_out,
              nsa_w_in, nsa_cmp_pos, nsa_cmp_w1, nsa_cmp_b1, nsa_cmp_w2, nsa_cmp_b2, nsa_w_out,
              gla_w_in, gla_w_g2, gla_b_g2, gla_norm, gla_w_out):
    h = x
    for layer in range(DEPTH):
        kind = layer % N_MIXERS
        j = layer // N_MIXERS
        hn = rmsnorm(h, mix_norm[layer])
        if kind == 0:
            y = rglru_mixer(hn, lru_w_in[j], lru_conv_w[j], lru_conv_b[j], lru_w_a[j], lru_b_a[j],
                            lru_w_x[j], lru_b_x[j], lru_lam[j], lru_w_out[j])
        elif kind == 1:
            y = nsa_mixer(hn, nsa_w_in[j], nsa_cmp_pos[j], nsa_cmp_w1[j], nsa_cmp_b1[j], nsa_cmp_w2[j],
                          nsa_cmp_b2[j], nsa_w_out[j], rel_bias)
        else:
            y = gla_mixer(hn, gla_w_in[j], gla_w_g2[j], gla_b_g2[j], gla_norm[j], gla_w_out[j])
        h = h + y
        h = h + swiglu(rmsnorm(h, ffn_norm[layer]), ffn_w_gate[layer], ffn_w_up[layer], ffn_w_down[layer])
    return rmsnorm(h, final_norm)
```

```python
import functools
import math

import numpy as np
import jax
import jax.numpy as jnp
from jax import lax
from jax.experimental import pallas as pl
from jax.experimental.pallas import tpu as pltpu

F32 = jnp.float32
BF16 = jnp.bfloat16

NORM_EPS = 1e-6
D_MODEL = 1024
D_FF = 2816

LRU_HEADS = 10
LRU_HEAD_DIM = 128
CONV_WIDTH = 4
LRU_C = 8.0

NSA_HEADS = 16
NSA_GROUPS = 4
NSA_HG = NSA_HEADS // NSA_GROUPS
NSA_DH = 64
CMP_LEN = 32
CMP_STRIDE = 16
SEL_LEN = 64
SEL_TOPN = 16
WINDOW = 512
FORCE_SCORE = 1e4
REL_BUCKETS = 32
REL_MAX_DIST = 128

GLA_HEADS = 4
GLA_DK = 512
GLA_DV = 1024
GLA_RANK = 16
GLA_TAU = 16.0
GLA_CHUNK = 64

LANES = 128
QT = 128
SEL_PAD = 128
NEG = -1e30
V7X_VMEM_BYTES = 64 * 1024 * 1024

NT_DIMS = (((1,), (1,)), ((), ()))
TN_DIMS = (((0,), (0,)), ((), ()))


def _cparams(semantics, vmem_mb):
    assert vmem_mb * 1024 * 1024 < V7X_VMEM_BYTES
    return pltpu.CompilerParams(dimension_semantics=semantics, vmem_limit_bytes=vmem_mb * 1024 * 1024)


def _resident(shape, index_map):
    return pl.BlockSpec(shape, index_map, pipeline_mode=pl.Buffered(1))


def _dot(a, b):
    return jnp.dot(a, b, preferred_element_type=F32)


def _split2(x):
    hi = x.astype(BF16)
    lo = (x - hi.astype(F32)).astype(BF16)
    return hi, lo


def _rmsnorm(x, g):
    ms = jnp.mean(x * x, axis=-1, keepdims=True)
    return x * lax.rsqrt(ms + NORM_EPS) * g


def _gelu_tanh(x):
    return x * (0.5 * (1.0 + jnp.tanh(math.sqrt(2.0 / math.pi) * (x + 0.044715 * (x * x * x)))))


def _softplus(z):
    return jnp.maximum(z, 0.0) + jnp.log1p(jnp.exp(-jnp.abs(z)))


def _norm_proj_body(x_ref, g_ref, w_ref, o_ref, *, n_chunk):
    xn = _rmsnorm(x_ref[...], g_ref[...]).astype(BF16)
    n = o_ref.shape[-1]
    for c0 in range(0, n, n_chunk):
        c1 = min(c0 + n_chunk, n)
        o_ref[:, c0:c1] = _dot(xn, w_ref[:, c0:c1])


def _norm_proj(h, g, w_bf16, *, tm=512, n_chunk=512):
    t, d = h.shape
    n = w_bf16.shape[1]
    assert t % tm == 0 and n % LANES == 0
    return pl.pallas_call(
        functools.partial(_norm_proj_body, n_chunk=n_chunk),
        grid=(t // tm,),
        in_specs=[pl.BlockSpec((tm, d), lambda i: (i, 0)),
                  _resident((1, d), lambda i: (0, 0)),
                  _resident((d, n), lambda i: (0, 0))],
        out_specs=pl.BlockSpec((tm, n), lambda i: (i, 0)),
        out_shape=jax.ShapeDtypeStruct((t, n), F32),
        compiler_params=_cparams(("parallel",), 40),
        name="norm_proj",
    )(h, g.reshape(1, d), w_bf16)


def _ffn_body(h_ref, y_ref, wo_ref, g_ref, wg_ref, wu_ref, wd_ref, *rest, ff_chunk, final):
    if final:
        fg_ref, o_ref, act_ref = rest
    else:
        o_ref, act_ref = rest
    h2 = h_ref[...] + _dot(y_ref[...], wo_ref[...])
    xn = _rmsnorm(h2, g_ref[...]).astype(BF16)
    dff = wg_ref.shape[1]
    for c0 in range(0, dff, ff_chunk):
        c1 = min(c0 + ff_chunk, dff)
        gate = _dot(xn, wg_ref[:, c0:c1])
        up = _dot(xn, wu_ref[:, c0:c1])
        act_ref[:, c0:c1] = (gate * jax.nn.sigmoid(gate) * up).astype(BF16)
    h3 = h2 + _dot(act_ref[...], wd_ref[...])
    if final:
        h3 = _rmsnorm(h3, fg_ref[...])
    o_ref[...] = h3


def _out_res_ffn(h, y_bf16, wo, g, wg, wu, wd, final_g=None, *, tm=512, ff_chunk=256):
    t, d = h.shape
    k = y_bf16.shape[1]
    dff = wg.shape[1]
    final = final_g is not None
    in_specs = [pl.BlockSpec((tm, d), lambda i: (i, 0)),
                pl.BlockSpec((tm, k), lambda i: (i, 0)),
                _resident((k, d), lambda i: (0, 0)),
                _resident((1, d), lambda i: (0, 0)),
                _resident((d, dff), lambda i: (0, 0)),
                _resident((d, dff), lambda i: (0, 0)),
                _resident((dff, d), lambda i: (0, 0))]
    args = [h, y_bf16, wo, g.reshape(1, d), wg, wu, wd]
    if final:
        in_specs.append(_resident((1, d), lambda i: (0, 0)))
        args.append(final_g.reshape(1, d))
    return pl.pallas_call(
        functools.partial(_ffn_body, ff_chunk=ff_chunk, final=final),
        grid=(t // tm,),
        in_specs=in_specs,
        out_specs=pl.BlockSpec((tm, d), lambda i: (i, 0)),
        out_shape=jax.ShapeDtypeStruct((t, d), F32),
        scratch_shapes=[pltpu.VMEM((tm, dff), BF16)],
        compiler_params=_cparams(("parallel",), 52),
        name="out_res_ffn",
    )(*args)


def _rglru_body(gate_ref, x_ref, cw_ref, cb_ref, wa_ref, ba_ref, wx_ref, bx_ref, lam_ref, y_ref,
                xbuf, a_s, u_s, carry):
    ts = x_ref.shape[1]

    @pl.when(pl.program_id(2) == 0)
    def _():
        xbuf[0:8, :] = jnp.zeros((8, LANES), F32)
        carry[...] = jnp.zeros_like(carry)

    x = x_ref[0]
    xbuf[8:ts + 8, :] = x
    cw = cw_ref[...]
    xc = (cb_ref[...] + xbuf[5:ts + 5, :] * cw[0:1] + xbuf[6:ts + 6, :] * cw[1:2]
          + xbuf[7:ts + 7, :] * cw[2:3] + x * cw[3:4])
    xbuf[0:8, :] = x[ts - 8:ts, :]

    xcb = xc.astype(BF16)
    r = jax.nn.sigmoid(_dot(xcb, wa_ref[0]) + ba_ref[...])
    i = jax.nn.sigmoid(_dot(xcb, wx_ref[0]) + bx_ref[...])
    log_a = (-LRU_C) * r * _softplus(-lam_ref[...])
    a = jnp.exp(log_a)
    u = jnp.sqrt(1.0 - a * a) * (i * xc)

    sub = lax.broadcasted_iota(jnp.int32, (ts, LANES), 0) & 7
    for s in (1, 2, 4):
        a_sh = pltpu.roll(a, s, axis=0)
        u_sh = pltpu.roll(u, s, axis=0)
        m = sub >= s
        u = jnp.where(m, u + a * u_sh, u)
        a = jnp.where(m, a * a_sh, a)
    a_s[...] = a
    u_s[...] = u

    def group(g, c):
        off = pl.multiple_of(g * 8, 8)
        hg = u_s[pl.ds(off, 8), :] + a_s[pl.ds(off, 8), :] * c
        u_s[pl.ds(off, 8), :] = hg
        return hg[7:8, :]

    carry[...] = lax.fori_loop(0, ts // 8, group, carry[...], unroll=8)
    y_ref[0] = (_gelu_tanh(gate_ref[0]) * u_s[...]).astype(BF16)


def _rglru_core(proj, conv_w, conv_b, w_a, b_a, w_x, b_x, lam, *, ts=512):
    b, s, _ = proj.shape
    nh, hd = LRU_HEADS, LRU_HEAD_DIM
    d_rnn = nh * hd
    row = lambda v: v.reshape(1, d_rnn)
    vec_spec = pl.BlockSpec((1, hd), lambda bi, hi, ti: (0, hi))
    return pl.pallas_call(
        _rglru_body,
        grid=(b, nh, s // ts),
        in_specs=[pl.BlockSpec((1, ts, hd), lambda bi, hi, ti: (bi, ti, hi)),
                  pl.BlockSpec((1, ts, hd), lambda bi, hi, ti: (bi, ti, nh + hi)),
                  pl.BlockSpec((CONV_WIDTH, hd), lambda bi, hi, ti: (0, hi)),
                  vec_spec,
                  pl.BlockSpec((1, hd, hd), lambda bi, hi, ti: (hi, 0, 0)),
                  vec_spec,
                  pl.BlockSpec((1, hd, hd), lambda bi, hi, ti: (hi, 0, 0)),
                  vec_spec,
                  vec_spec],
        out_specs=pl.BlockSpec((1, ts, hd), lambda bi, hi, ti: (bi, ti, hi)),
        out_shape=jax.ShapeDtypeStruct((b, s, d_rnn), BF16),
        scratch_shapes=[pltpu.VMEM((ts + 8, hd), F32), pltpu.VMEM((ts, hd), F32),
                        pltpu.VMEM((ts, hd), F32), pltpu.VMEM((1, hd), F32)],
        compiler_params=_cparams(("parallel", "parallel", "arbitrary"), 32),
        name="rglru_core",
    )(proj, proj, conv_w, row(conv_b), w_a.astype(BF16), row(b_a), w_x.astype(BF16), row(b_x), row(lam))


def _bucket_upper_bounds():
    n = np.arange(0, 2 * REL_MAX_DIST + 1)
    max_exact = REL_BUCKETS // 2
    nf = np.maximum(n, 1).astype(np.float32)
    large = max_exact + (np.log(nf / np.float32(max_exact)) / np.float32(math.log(REL_MAX_DIST / max_exact))
                         * np.float32(REL_BUCKETS - max_exact)).astype(np.int32)
    bucket = np.where(n < max_exact, n, np.minimum(large, REL_BUCKETS - 1))
    assert np.all(np.diff(bucket) >= 0) and bucket[REL_MAX_DIST] == REL_BUCKETS - 1
    return [int(np.max(n[bucket <= b])) for b in range(REL_BUCKETS - 1)]


def _bias_tables_body(rb_ref, dall_ref, caug_ref, *, upper):
    h = pl.program_id(0)
    last = rb_ref[REL_BUCKETS - 1, h]

    def lookup(dist):
        v = jnp.full(dist.shape, last, F32)
        for b in range(REL_BUCKETS - 2, -1, -1):
            v = jnp.where(dist <= upper[b], rb_ref[b, h], v)
        return v

    qi = lax.broadcasted_iota(jnp.int32, (QT, QT), 0)
    kj = lax.broadcasted_iota(jnp.int32, (QT, QT), 1)
    const = jnp.full((QT, QT), last, F32)
    dall_ref[0, 0] = jnp.where(qi >= kj, lookup(qi - kj), NEG)
    dall_ref[1, 0] = lookup(QT + qi - kj)
    dall_ref[2, 0] = const
    dall_ref[3, 0] = jnp.where(kj > qi, const, NEG)
    dc = qi - CMP_STRIDE * kj + (CMP_STRIDE * 16 - CMP_LEN + 1)
    near = jnp.where(dc >= 0, lookup(dc), NEG)
    caug_ref[0] = jnp.where(kj < 32, near, jnp.where(kj == 32, const, jnp.where(kj == 33, NEG, 0.0)))


def _bias_tables(rel_bias):
    assert QT == 128 and WINDOW == 4 * QT and REL_MAX_DIST <= QT
    return pl.pallas_call(
        functools.partial(_bias_tables_body, upper=_bucket_upper_bounds()),
        grid=(NSA_HEADS,),
        in_specs=[pl.BlockSpec(memory_space=pltpu.SMEM)],
        out_specs=[pl.BlockSpec((4, 1, QT, QT), lambda h: (0, h, 0, 0)),
                   pl.BlockSpec((1, QT, QT), lambda h: (h, 0, 0))],
        out_shape=[jax.ShapeDtypeStruct((4, NSA_HEADS, QT, QT), F32),
                   jax.ShapeDtypeStruct((NSA_HEADS, QT, QT), F32)],
        compiler_params=_cparams(("parallel",), 16),
        name="nsa_bias_tables",
    )(rel_bias)


def _compress_body(x_ref, w1_ref, pos_ref, b1_ref, w2_ref, b2_ref, o_ref):
    w1 = w1_ref[...]
    half = w1.shape[1] // 2
    ab = _dot(x_ref[0].astype(BF16), w1)
    pc = _dot(pos_ref[...], w1)
    c0 = pc[0:1, :half] + pc[1:2, :half] + pc[2:3, half:] + pc[3:4, half:]
    nb = ab.shape[0]
    pre = ab[:, :half] + pltpu.roll(ab[:, half:], nb - 1, axis=0) + c0 + b1_ref[...]
    o_ref[0] = _dot(_gelu_tanh(pre).astype(BF16), w2_ref[...]) + b2_ref[...]


def _nsa_compress(x, pos, w1, b1, w2, b2):
    b, s, gd = x.shape
    g, dh = NSA_GROUPS, NSA_DH
    half = CMP_LEN // 2
    assert CMP_STRIDE == half and s % CMP_STRIDE == 0
    nb = s // CMP_STRIDE
    xr = x.reshape(b, nb, half * gd)
    w1r = w1.reshape(2, half, dh, dh)
    eye = jnp.eye(g, dtype=F32)
    w1big = jnp.einsum('hldn,gk->lgdhkn', w1r, eye).reshape(half * gd, 2 * gd).astype(BF16)
    posr = jnp.broadcast_to(pos.reshape(2, half, 1, dh), (2, half, g, dh)).reshape(2, half * gd)
    pa_hi, pa_lo = _split2(posr[0:1])
    pb_hi, pb_lo = _split2(posr[1:2])
    pos_rows = jnp.concatenate([pa_hi, pa_lo, pb_hi, pb_lo, jnp.zeros((4, half * gd), BF16)], axis=0)
    w2big = jnp.einsum('dn,gk->gdkn', w2, eye).reshape(gd, gd).astype(BF16)
    tile = lambda v: jnp.tile(v.reshape(1, dh), (1, g))
    return pl.pallas_call(
        _compress_body,
        grid=(b,),
        in_specs=[pl.BlockSpec((1, nb, half * gd), lambda i: (i, 0, 0)),
                  _resident((half * gd, 2 * gd), lambda i: (0, 0)),
                  _resident((8, half * gd), lambda i: (0, 0)),
                  _resident((1, gd), lambda i: (0, 0)),
                  _resident((gd, gd), lambda i: (0, 0)),
                  _resident((1, gd), lambda i: (0, 0))],
        out_specs=pl.BlockSpec((1, nb, gd), lambda i: (i, 0, 0)),
        out_shape=jax.ShapeDtypeStruct((b, nb, gd), F32),
        compiler_params=_cparams(("parallel",), 40),
        name="nsa_compress",
    )(xr, w1big, pos_rows, tile(b1), w2big, tile(b2))


def _nsa_attn_body(q_ref, gt_ref, kct_ref, vca_ref, kst_ref, vsa_ref, kwt_ref, vwa_ref,
                   dall_ref, caug_ref, selt_ref, o_ref, m_s, acc_s, *, n_sel):
    g = pl.program_id(1)
    qt = pl.program_id(2)
    hg, dh = NSA_HG, NSA_DH
    rows = hg * QT
    q = q_ref[0] * (dh ** -0.5)
    qs = jnp.concatenate([q[:, h * dh:(h + 1) * dh] for h in range(hg)], axis=0).astype(BF16)

    nbp = kct_ref.shape[-1]
    lane = lax.broadcasted_iota(jnp.int32, (LANES, nbp), 1)
    rowc = lax.broadcasted_iota(jnp.int32, (LANES, nbp), 0)
    base = 8 * qt - 16
    near = jnp.where(lane == base + rowc, 1.0, 0.0)
    far = jnp.where(lane < base, 1.0, 0.0)
    future = jnp.where(lane >= base + 32, 1.0, 0.0)
    shift = jnp.where(rowc < 32, near, jnp.where(rowc == 32, far, jnp.where(rowc == 33, future, 0.0))).astype(BF16)
    ca_hi, ca_lo = _split2(caug_ref[...].reshape(rows, LANES))
    s = _dot(qs, kct_ref[0, 0]) + _dot(ca_hi, shift) + _dot(ca_lo, shift)
    m = jnp.max(s, axis=-1, keepdims=True)
    e = jnp.exp(s - m)
    acc = _dot(e.astype(BF16), vca_ref[0, 0])
    inv = jnp.where(m > 0.1 * NEG, 1.0 / acc[:, dh:dh + 1], 0.0)
    o_c = acc[:, :dh] * inv
    pn = e * inv
    psum = pn[0:QT]
    for h in range(1, hg):
        psum = psum + pn[h * QT:(h + 1) * QT]

    p_hi, p_lo = _split2(psum)
    selt = selt_ref[...]
    imp = (lax.dot_general(selt, p_hi, NT_DIMS, preferred_element_type=F32)
           + lax.dot_general(selt, p_lo, NT_DIMS, preferred_element_type=F32))
    mrow = lax.broadcasted_iota(jnp.int32, (SEL_PAD, QT), 0)
    t = qt * QT + lax.broadcasted_iota(jnp.int32, (SEL_PAD, QT), 1)
    cur = jnp.right_shift(t, 6)
    valid = mrow * SEL_LEN <= t
    forced = jnp.where(mrow == 0, FORCE_SCORE,
                       jnp.where(mrow == cur, FORCE_SCORE, jnp.where(mrow == cur - 1, FORCE_SCORE, 0.0)))
    score = jnp.where(valid, imp + forced, -jnp.inf)
    mrow_f = mrow.astype(F32)
    sel = jnp.zeros((SEL_PAD, QT), F32)
    for _ in range(n_sel):
        mx = jnp.max(score, axis=0, keepdims=True)
        first = jnp.min(jnp.where(score == mx, mrow_f, float(SEL_PAD)), axis=0, keepdims=True)
        pick = mrow_f == first
        sel = jnp.where(pick, 1.0, sel)
        score = jnp.where(pick, -jnp.inf, score)
    msel = jnp.where(valid, sel, 0.0).T.astype(BF16)

    def tile_step(kt, kt_ref, va_ref, bias4, addmask):
        col = pl.multiple_of(kt * QT, QT)
        sc = _dot(qs, kt_ref[0, 0, :, pl.ds(col, QT)]).reshape(hg, QT, QT) + bias4
        if addmask is not None:
            sc = sc + addmask[None]
        sc = sc.reshape(rows, QT)
        m_prev = m_s[...]
        m_new = jnp.maximum(m_prev, jnp.max(sc, axis=-1, keepdims=True))
        p = jnp.exp(sc - m_new)
        acc_s[...] = acc_s[...] * jnp.exp(m_prev - m_new) + _dot(p.astype(BF16), va_ref[0, 0, pl.ds(col, QT), :])
        m_s[...] = m_new

    def finish():
        a = acc_s[...]
        return a[:, :dh] / a[:, dh:dh + 1]

    def reset():
        m_s[...] = jnp.full(m_s.shape, NEG, F32)
        acc_s[...] = jnp.zeros(acc_s.shape, F32)

    mi = lax.broadcasted_iota(jnp.int32, (SEL_PAD, QT), 0)
    kblk = jnp.right_shift(lax.broadcasted_iota(jnp.int32, (SEL_PAD, QT), 1), 6)

    def sel_step(kt):
        expand = jnp.where(mi == 2 * kt + kblk, 1.0, 0.0).astype(BF16)
        addmask = (_dot(msel, expand) - 1.0) * (-NEG)
        tile_step(kt, kst_ref, vsa_ref, dall_ref[jnp.minimum(qt - kt, 2)], addmask)

    reset()
    sel_step(qt)

    def sel_loop(kt, carry):
        sel_step(kt)
        return carry

    lax.fori_loop(0, qt, sel_loop, 0)
    o_s = finish()

    reset()
    for off, tab in ((0, 0), (1, 1), (2, 2), (3, 2), (4, 3)):
        if off == 0:
            tile_step(qt, kwt_ref, vwa_ref, dall_ref[tab], None)
        else:
            @pl.when(qt >= off)
            def _():
                tile_step(qt - off, kwt_ref, vwa_ref, dall_ref[tab], None)
    o_w = finish()

    gi = lax.broadcasted_iota(jnp.int32, (LANES, LANES), 0)
    gj = lax.broadcasted_iota(jnp.int32, (LANES, LANES), 1)
    pick_cols = jnp.where(gi == g * (3 * hg) + gj, 1.0, 0.0).astype(BF16)
    g_hi, g_lo = _split2(jax.nn.sigmoid(gt_ref[0]))
    gates = _dot(g_hi, pick_cols) + _dot(g_lo, pick_cols)
    outs = []
    for h in range(hg):
        r0, r1 = h * QT, (h + 1) * QT
        outs.append(gates[:, 3 * h:3 * h + 1] * o_c[r0:r1] + gates[:, 3 * h + 1:3 * h + 2] * o_s[r0:r1]
                    + gates[:, 3 * h + 2:3 * h + 3] * o_w[r0:r1])
    o_ref[0] = jnp.concatenate(outs, axis=-1).astype(BF16)


def _sel_map_t(nb_cmp_pad):
    i = np.arange(nb_cmp_pad)[None, :] * CMP_STRIDE
    m = np.arange(SEL_PAD)[:, None] * SEL_LEN
    return ((i < m + SEL_LEN) & (i + CMP_LEN > m)).astype(np.float32)


def _nsa_attention(proj, kc, vc, dall, caug):
    b, s, _ = proj.shape
    g, hg, dh = NSA_GROUPS, NSA_HG, NSA_DH
    qw, kvw = NSA_HEADS * dh, g * dh
    assert s % QT == 0 and s // SEL_LEN <= SEL_PAD
    nbp = s // CMP_STRIDE

    def keys_t(x):
        return jnp.transpose(x.reshape(b, -1, g, dh), (0, 2, 3, 1)).astype(BF16)

    def vals_aug(x):
        n = x.shape[1]
        v = jnp.transpose(x.reshape(b, n, g, dh), (0, 2, 1, 3))
        pad = jnp.zeros((b, g, n, LANES - dh - 1), F32)
        return jnp.concatenate([v, jnp.ones((b, g, n, 1), F32), pad], axis=-1).astype(BF16)

    kv = lambda j: proj[:, :, qw + j * kvw: qw + (j + 1) * kvw]
    kct, vca = keys_t(kc), vals_aug(vc)
    kst, vsa = keys_t(kv(2)), vals_aug(kv(3))
    kwt, vwa = keys_t(kv(4)), vals_aug(kv(5))
    selt = jnp.asarray(_sel_map_t(nbp), BF16)
    gate_blk = (qw + 6 * kvw) // LANES
    assert (qw + 6 * kvw) % LANES == 0 and 3 * NSA_HEADS <= LANES
    n_sel = min(SEL_TOPN, s // SEL_LEN)
    per_bg = lambda shape: pl.BlockSpec((1, 1) + shape, lambda bi, gi, ti: (bi, gi, 0, 0))
    return pl.pallas_call(
        functools.partial(_nsa_attn_body, n_sel=n_sel),
        grid=(b, g, s // QT),
        in_specs=[pl.BlockSpec((1, QT, hg * dh), lambda bi, gi, ti: (bi, ti, gi)),
                  pl.BlockSpec((1, QT, LANES), lambda bi, gi, ti: (bi, ti, gate_blk)),
                  per_bg((dh, nbp)), per_bg((nbp, LANES)),
                  per_bg((dh, s)), per_bg((s, LANES)),
                  per_bg((dh, s)), per_bg((s, LANES)),
                  pl.BlockSpec((4, hg, QT, QT), lambda bi, gi, ti: (0, gi, 0, 0)),
                  pl.BlockSpec((hg, QT, QT), lambda bi, gi, ti: (gi, 0, 0)),
                  _resident((SEL_PAD, nbp), lambda bi, gi, ti: (0, 0))],
        out_specs=pl.BlockSpec((1, QT, hg * dh), lambda bi, gi, ti: (bi, ti, gi)),
        out_shape=jax.ShapeDtypeStruct((b, s, NSA_HEADS * dh), BF16),
        scratch_shapes=[pltpu.VMEM((hg * QT, LANES), F32), pltpu.VMEM((hg * QT, LANES), F32)],
        compiler_params=_cparams(("parallel", "parallel", "arbitrary"), 40),
        name="nsa_attention",
    )(proj, proj, kct, vca, kst, vsa, kwt, vwa, dall, caug, selt)


def _gla_body(q_ref, k_ref, v_ref, r_ref, gl_ref, wg2_ref, bg2_ref, ng_ref, o_ref, state):
    ct = q_ref.shape[1]
    ch = GLA_CHUNK
    nch = ct // ch
    dk = q_ref.shape[2]

    @pl.when(pl.program_id(2) == 0)
    def _():
        state[...] = jnp.zeros_like(state)

    gl_hi, gl_lo = _split2(gl_ref[0])
    w_hi, w_lo = _split2(wg2_ref[...])
    z = _dot(gl_hi, w_hi) + _dot(gl_hi, w_lo) + _dot(gl_lo, w_hi) + bg2_ref[...]
    la = -_softplus(-z) * (1.0 / GLA_TAU)

    ri = lax.broadcasted_iota(jnp.int32, (ct, ct), 0)
    ci = lax.broadcasted_iota(jnp.int32, (ct, ct), 1)
    tri_f = jnp.where(jnp.right_shift(ri, 6) == jnp.right_shift(ci, 6), jnp.where(ci <= ri, 1.0, 0.0), 0.0)
    tri = tri_f > 0.0
    tri_b = tri_f.astype(BF16)
    la_hi, la_lo = _split2(la)
    bcum = _dot(tri_b, la_hi) + _dot(tri_b, la_lo)
    b3 = bcum.reshape(nch, ch, dk)
    blast = b3[:, ch - 1:ch, :]

    k = k_ref[0]
    q_dec = (q_ref[0] * (dk ** -0.5) * jnp.exp(bcum)).astype(BF16)
    k_inv = (k * jnp.exp(-bcum)).astype(BF16)
    k_end = (k.reshape(nch, ch, dk) * jnp.exp(blast - b3)).reshape(ct, dk).astype(BF16)
    v_b = v_ref[0].astype(BF16)
    att = lax.dot_general(q_dec, k_inv, NT_DIMS, preferred_element_type=F32)
    o_intra = _dot(jnp.where(tri, att, 0.0).astype(BF16), v_b)
    dec = jnp.exp(blast)

    st = state[...]
    outs = []
    for j in range(nch):
        sl = slice(j * ch, (j + 1) * ch)
        outs.append(o_intra[sl] + lax.dot_general(q_dec[sl], st.astype(BF16), NT_DIMS,
                                                  preferred_element_type=F32))
        st = st * dec[j] + lax.dot_general(v_b[sl], k_end[sl], TN_DIMS, preferred_element_type=F32)
    state[...] = st
    o = jnp.concatenate(outs, axis=0)
    r = r_ref[0]
    o_ref[0] = (_rmsnorm(o, ng_ref[...]) * (r * jax.nn.sigmoid(r))).astype(BF16)


def _gla_core(proj, w_g2, b_g2, norm_g, *, ct=512):
    b, s, _ = proj.shape
    nh = GLA_HEADS
    dk, dv = GLA_DK // nh, GLA_DV // nh
    assert dk == LANES and dv == 2 * LANES and ct % GLA_CHUNK == 0 and s % ct == 0
    kblk, vblk, rblk = GLA_DK // dk, 2 * GLA_DK // dv, (2 * GLA_DK + GLA_DV) // dv
    glblk = (2 * GLA_DK + 2 * GLA_DV) // LANES
    wg2_pad = jnp.zeros((LANES, GLA_DK), F32).at[:GLA_RANK].set(w_g2)
    return pl.pallas_call(
        _gla_body,
        grid=(b, nh, s // ct),
        in_specs=[pl.BlockSpec((1, ct, dk), lambda bi, hi, ti: (bi, ti, hi)),
                  pl.BlockSpec((1, ct, dk), lambda bi, hi, ti: (bi, ti, kblk + hi)),
                  pl.BlockSpec((1, ct, dv), lambda bi, hi, ti: (bi, ti, vblk + hi)),
                  pl.BlockSpec((1, ct, dv), lambda bi, hi, ti: (bi, ti, rblk + hi)),
                  pl.BlockSpec((1, ct, LANES), lambda bi, hi, ti: (bi, ti, glblk)),
                  pl.BlockSpec((LANES, dk), lambda bi, hi, ti: (0, hi)),
                  pl.BlockSpec((1, dk), lambda bi, hi, ti: (0, hi)),
                  _resident((1, dv), lambda bi, hi, ti: (0, 0))],
        out_specs=pl.BlockSpec((1, ct, dv), lambda bi, hi, ti: (bi, ti, hi)),
        out_shape=jax.ShapeDtypeStruct((b, s, GLA_DV), BF16),
        scratch_shapes=[pltpu.VMEM((dv, dk), F32)],
        compiler_params=_cparams(("parallel", "parallel", "arbitrary"), 32),
        name="gla_core",
    )(proj, proj, proj, proj, proj, wg2_pad, b_g2.reshape(1, GLA_DK), norm_g.reshape(1, dv))


def _pad_cols(w, mult=LANES):
    n = w.shape[1]
    npad = -(-n // mult) * mult
    return w if npad == n else jnp.pad(w, ((0, 0), (0, npad - n)))


def kernel(x, rel_bias, final_norm, mix_norm, ffn_norm, ffn_w_gate, ffn_w_up, ffn_w_down, lru_w_in, lru_conv_w, lru_conv_b, lru_w_a, lru_b_a, lru_w_x, lru_b_x, lru_lam, lru_w_out, nsa_w_in, nsa_cmp_pos, nsa_cmp_w1, nsa_cmp_b1, nsa_cmp_w2, nsa_cmp_b2, nsa_w_out, gla_w_in, gla_w_g2, gla_b_g2, gla_norm, gla_w_out):
    b, s, d = x.shape
    depth = mix_norm.shape[0]
    h = x.reshape(b * s, d)
    dall, caug = _bias_tables(rel_bias)
    for layer in range(depth):
        kind, j = layer % 3, layer // 3
        if kind == 0:
            proj = _norm_proj(h, mix_norm[layer], lru_w_in[j].astype(BF16))
            y = _rglru_core(proj.reshape(b, s, -1), lru_conv_w[j], lru_conv_b[j], lru_w_a[j], lru_b_a[j],
                            lru_w_x[j], lru_b_x[j], lru_lam[j])
            w_out = lru_w_out[j]
        elif kind == 1:
            proj = _norm_proj(h, mix_norm[layer], _pad_cols(nsa_w_in[j]).astype(BF16)).reshape(b, s, -1)
            qw, kvw = NSA_HEADS * NSA_DH, NSA_GROUPS * NSA_DH
            kc = _nsa_compress(proj[:, :, qw:qw + kvw], nsa_cmp_pos[j, 0], nsa_cmp_w1[j, 0], nsa_cmp_b1[j, 0],
                               nsa_cmp_w2[j, 0], nsa_cmp_b2[j, 0])
            vc = _nsa_compress(proj[:, :, qw + kvw:qw + 2 * kvw], nsa_cmp_pos[j, 1], nsa_cmp_w1[j, 1],
                               nsa_cmp_b1[j, 1], nsa_cmp_w2[j, 1], nsa_cmp_b2[j, 1])
            y = _nsa_attention(proj, kc, vc, dall, caug)
            w_out = nsa_w_out[j]
        else:
            proj = _norm_proj(h, mix_norm[layer], _pad_cols(gla_w_in[j]).astype(BF16)).reshape(b, s, -1)
            y = _gla_core(proj, gla_w_g2[j], gla_b_g2[j], gla_norm[j])
            w_out = gla_w_out[j]
        h = _out_res_ffn(h, y.reshape(b * s, -1), w_out.astype(BF16), ffn_norm[layer],
                         ffn_w_gate[layer].astype(BF16), ffn_w_up[layer].astype(BF16),
                         ffn_w_down[layer].astype(BF16),
                         final_norm if layer == depth - 1 else None)
    return h.reshape(b, s, d)
```
